```python
import math
import jax, jax.numpy as jnp
from jax import lax
import numpy as np

D_MODEL = 1024
BATCH = 2
SEQ = 8192
DEPTH = 2

GRID_W = 64
CTX_LEN = 256
N_MIXERS = 2
N_HYENA = (DEPTH + 1) // 2
N_MLA = DEPTH // 2
ALPHA = (2.0 * DEPTH) ** 0.25
BETA = (8.0 * DEPTH) ** -0.25
LN_EPS = 1e-6
RMS_EPS = 1e-6
HY_ORDER = 2
HY_EMB = 33
HY_FILT_W = 64
HY_TARGET = 1e-2
HY_FAST = 0.3
HY_SLOW = 1.5
MLA_HEADS = 8
Q_LORA = 384
KV_LORA = 256
QK_NOPE = 128
QK_ROPE = 64
V_DIM = 128
ROPE_BASE = 10000.0
ATTN_SCALE = (QK_NOPE + QK_ROPE) ** -0.5
Q_BLOCK = 128
D_FF = -(-8 * D_MODEL // (3 * 256)) * 256

kernel_name = "hybrid_hyena_mla_dit_block"


def layer_norm(x, g, b):
    xf = x.astype(jnp.float32)
    mu = xf.mean(-1, keepdims=True)
    var = jnp.square(xf - mu).mean(-1, keepdims=True)
    return ((xf - mu) * lax.rsqrt(var + LN_EPS) * g + b).astype(x.dtype)


def rms_norm(x, g):
    xf = x.astype(jnp.float32)
    return (xf * lax.rsqrt(jnp.square(xf).mean(-1, keepdims=True) + RMS_EPS) * g).astype(x.dtype)


def ada(cond, w_mod, b_mod):
    m = jax.nn.silu(cond) @ w_mod + b_mod
    return jnp.split(m[..., None, :], 6, axis=-1)


def swiglu(h, wg, wu, wd):
    return (jax.nn.silu(h @ wg) * (h @ wu)) @ wd


def short_conv(u, w, b):
    L = u.shape[1]
    up = jnp.pad(u, ((0, 0), (1, 1), (0, 0)))
    return up[:, :L] * w[0] + up[:, 1:L + 1] * w[1] + up[:, 2:] * w[2] + b


def hyena_filters(L, fw1, fb1, fw2, fb2, fw3, freq):
    t = jnp.linspace(0.0, 1.0, L, dtype=jnp.float32)[:, None]
    bands = (HY_EMB - 1) // 2
    w = 2.0 * math.pi * jnp.arange(L, dtype=jnp.float32)[:, None] / L
    f = jnp.linspace(1e-4, bands - 1, bands, dtype=jnp.float32)[None, :]
    feats = jnp.concatenate([t, jnp.cos(f * w), -jnp.sin(f * w)], axis=-1)
    hdn = jnp.sin(freq[0] * (feats @ fw1 + fb1))
    hdn = jnp.sin(freq[1] * (hdn @ fw2 + fb2))
    h = (hdn @ fw3).reshape(L, HY_ORDER, 2, D_MODEL)
    deltas = jnp.linspace(math.log(HY_TARGET) / HY_SLOW, math.log(HY_TARGET) / HY_FAST,
                          D_MODEL, dtype=jnp.float32)
    decay = jnp.exp(-t * jnp.abs(deltas))
    return h * decay[:, None, None, :]


def bidir_long_conv(z, h_fwd, h_bwd, bias):
    L = z.shape[1]
    k = jnp.concatenate([h_fwd, jnp.zeros_like(h_fwd[:1]), h_bwd[1:][::-1]], axis=0)
    kf = jnp.fft.rfft(k.astype(jnp.float32), n=2 * L, axis=0)
    zf = jnp.fft.rfft(z.astype(jnp.float32), n=2 * L, axis=1)
    y = jnp.fft.irfft(zf * kf, n=2 * L, axis=1)[:, :L]
    return (y + z.astype(jnp.float32) * bias).astype(z.dtype)


def hyena_mixer(h, w_in, b_in, conv_w, conv_b, fw1, fb1, fw2, fb2, fw3, freq, fbias, w_out, b_out):
    L = h.shape[1]
    u = short_conv(h @ w_in + b_in, conv_w, conv_b)
    x1, x2, v = jnp.split(u, 3, axis=-1)
    filt = hyena_filters(L, fw1, fb1, fw2, fb2, fw3, freq)
    z = v
    for n, gate in enumerate((x1, x2)):
        z = gate * bidir_long_conv(z, filt[:, n, 0], filt[:, n, 1], fbias[n])
    return z @ w_out + b_out


def axial_rope_tables(rows):
    row = jnp.repeat(jnp.arange(rows), GRID_W).astype(jnp.float32)
    col = jnp.tile(jnp.arange(GRID_W), rows).astype(jnp.float32)
    n = QK_ROPE // 4
    inv = ROPE_BASE ** (-jnp.arange(n, dtype=jnp.float32) / n)
    ang = jnp.concatenate([row[:, None] * inv, col[:, None] * inv], axis=-1)
    return jnp.cos(ang), jnp.sin(ang)


def apply_rope(x, cos, sin):
    x1, x2 = jnp.split(x, 2, axis=-1)
    return jnp.concatenate([x1 * cos - x2 * sin, x1 * sin + x2 * cos], axis=-1)


def mla_queries(h, wdq, gq, wuq, rope):
    B, L, _ = h.shape
    q = (rms_norm(h @ wdq, gq) @ wuq).reshape(B, L, MLA_HEADS, QK_NOPE + QK_ROPE)
    q_nope, q_rope = q[..., :QK_NOPE], q[..., QK_NOPE:]
    if rope is not None:
        cos, sin = rope
        q_rope = apply_rope(q_rope, cos[:, None, :], sin[:, None, :])
    return q_nope, q_rope


def mla_keys_values(h, wdkv, gkv, wkr, wuk, wuv, rope):
    B, L, _ = h.shape
    ckv = rms_norm(h @ wdkv, gkv)
    k_nope = (ckv @ wuk).reshape(B, L, MLA_HEADS, QK_NOPE)
    v = (ckv @ wuv).reshape(B, L, MLA_HEADS, V_DIM)
    k_rope = h @ wkr
    if rope is not None:
        cos, sin = rope
        k_rope = apply_rope(k_rope, cos, sin)
    return k_nope, k_rope, v


def attend(qn, qr, kn, kr, v):
    s = jnp.einsum('bqhd,bkhd->bhqk', qn, kn) + jnp.einsum('bqhr,bkr->bhqk', qr, kr)
    p = jax.nn.softmax(s.astype(jnp.float32) * ATTN_SCALE, axis=-1).astype(v.dtype)
    return jnp.einsum('bhqk,bkhd->bqhd', p, v)


def attend_blocks(qn, qr, kn, kr, v):
    B, L, H, _ = qn.shape
    nb = L // Q_BLOCK
    qn_b = jnp.moveaxis(qn.reshape(B, nb, Q_BLOCK, H, QK_NOPE), 1, 0)
    qr_b = jnp.moveaxis(qr.reshape(B, nb, Q_BLOCK, H, QK_ROPE), 1, 0)
    out = lax.map(lambda a: attend(a[0], a[1], kn, kr, v), (qn_b, qr_b))
    return jnp.moveaxis(out, 0, 1).reshape(B, L, H * V_DIM)


def setup_inputs(seed: int = 0) -> dict:
    key = jax.random.key(seed)
    ks = iter(jax.random.split(key, 48))

    def nrm(shape, std):
        return std * jax.random.normal(next(ks), shape, jnp.float32)

    D = D_MODEL
    H = MLA_HEADS
    return {
        "x": nrm((BATCH, SEQ, D), 1.0),
        "c": nrm((BATCH, D), 1.0),
        "ctx": nrm((BATCH, CTX_LEN, D), 1.0),
        "c_ctx": nrm((D,), 1.0),
        "mod_w": nrm((DEPTH, D, 6 * D), 0.5 * D ** -0.5),
        "mod_b": nrm((DEPTH, 6 * D), 0.02),
        "ln_g": 1.0 + nrm((DEPTH, 2, D), 0.02),
        "ln_b": nrm((DEPTH, 2, D), 0.02),
        "hy_w_in": nrm((N_HYENA, D, 3 * D), D ** -0.5),
        "hy_b_in": nrm((N_HYENA, 3 * D), 0.02),
        "hy_conv_w": nrm((N_HYENA, 3, 3 * D), 3 ** -0.5),
        "hy_conv_b": nrm((N_HYENA, 3 * D), 0.02),
        "hy_fw1": nrm((N_HYENA, HY_EMB, HY_FILT_W), HY_EMB ** -0.5),
        "hy_fb1": nrm((N_HYENA, HY_FILT_W), 0.02),
        "hy_fw2": nrm((N_HYENA, HY_FILT_W, HY_FILT_W), HY_FILT_W ** -0.5),
        "hy_fb2": nrm((N_HYENA, HY_FILT_W), 0.02),
        "hy_fw3": nrm((N_HYENA, HY_FILT_W, HY_ORDER * 2 * D), 0.02 * HY_FILT_W ** -0.5),
        "hy_freq": 1.0 + nrm((N_HYENA, 2, HY_FILT_W), 0.02),
        "hy_fbias": nrm((N_HYENA, HY_ORDER, D), 0.5),
        "hy_w_out": nrm((N_HYENA, D, D), BETA * D ** -0.5),
        "hy_b_out": nrm((N_HYENA, D), 0.02),
        "mla_wdq": nrm((N_MLA, D, Q_LORA), D ** -0.5),
        "mla_gq": 1.0 + nrm((N_MLA, Q_LORA), 0.02),
        "mla_wuq": nrm((N_MLA, Q_LORA, H * (QK_NOPE + QK_ROPE)), Q_LORA ** -0.5),
        "mla_wdkv": nrm((N_MLA, D, KV_LORA), D ** -0.5),
        "mla_gkv": 1.0 + nrm((N_MLA, KV_LORA), 0.02),
        "mla_wkr": nrm((N_MLA, D, QK_ROPE), D ** -0.5),
        "mla_wuk": nrm((N_MLA, KV_LORA, H * QK_NOPE), KV_LORA ** -0.5),
        "mla_wuv": nrm((N_MLA, KV_LORA, H * V_DIM), KV_LORA ** -0.5),
        "mla_wo": nrm((N_MLA, H * V_DIM, D), BETA * (H * V_DIM) ** -0.5),
        "ffn_wg": nrm((DEPTH, D, D_FF), D ** -0.5),
        "ffn_wu": nrm((DEPTH, D, D_FF), D ** -0.5),
        "ffn_wd": nrm((DEPTH, D_FF, D), BETA * D_FF ** -0.5),
    }


def reference(x, c, ctx, c_ctx, mod_w, mod_b, ln_g, ln_b,
              hy_w_in, hy_b_in, hy_conv_w, hy_conv_b, hy_fw1, hy_fb1, hy_fw2, hy_fb2,
              hy_fw3, hy_freq, hy_fbias, hy_w_out, hy_b_out,
              mla_wdq, mla_gq, mla_wuq, mla_wdkv, mla_gkv, mla_wkr, mla_wuk, mla_wuv, mla_wo,
              ffn_wg, ffn_wu, ffn_wd):
    B, L, _ = x.shape
    rows = L // GRID_W
    rope = axial_rope_tables(rows)
    xs, cs = x, ctx
    for i in range(DEPTH):
        last = i == DEPTH - 1
        j = i // N_MIXERS
        use_hyena = i % N_MIXERS == 0
        ctx_needed = (not last) or (not use_hyena)
        sh1, sc1, g1, sh2, sc2, g2 = ada(c, mod_w[i], mod_b[i])
        hx = xs * (1.0 + sc1) + sh1
        if ctx_needed:
            csh1, csc1, cg1, csh2, csc2, cg2 = ada(c_ctx, mod_w[i], mod_b[i])
            hc = cs * (1.0 + csc1) + csh1
        if use_hyena:
            hp = (hy_w_in[j], hy_b_in[j], hy_conv_w[j], hy_conv_b[j], hy_fw1[j], hy_fb1[j],
                  hy_fw2[j], hy_fb2[j], hy_fw3[j], hy_freq[j], hy_fbias[j], hy_w_out[j], hy_b_out[j])
            yx = hyena_mixer(hx, *hp)
            yc = None if last else hyena_mixer(hc, *hp)
        else:
            kvp = (mla_wdkv[j], mla_gkv[j], mla_wkr[j], mla_wuk[j], mla_wuv[j])
            qp = (mla_wdq[j], mla_gq[j], mla_wuq[j])
            kc_n, kc_r, vc = mla_keys_values(hc, *kvp, None)
            kx_n, kx_r, vx = mla_keys_values(hx, *kvp, rope)
            qx_n, qx_r = mla_queries(hx, *qp, rope)
            k_n = jnp.concatenate([kc_n, kx_n], axis=1)
            k_r = jnp.concatenate([kc_r, kx_r], axis=1)
            v_all = jnp.concatenate([vc, vx], axis=1)
            yx = attend_blocks(qx_n, qx_r, k_n, k_r, v_all) @ mla_wo[j]
            yc = None
            if not last:
                qc_n, qc_r = mla_queries(hc, *qp, None)
                yc = attend(qc_n, qc_r, kc_n, kc_r, vc).reshape(B, cs.shape[1], MLA_HEADS * V_DIM) @ mla_wo[j]
        xs = layer_norm(ALPHA * xs + g1 * yx, ln_g[i, 0], ln_b[i, 0])
        xs = layer_norm(ALPHA * xs + g2 * swiglu(xs * (1.0 + sc2) + sh2, ffn_wg[i], ffn_wu[i], ffn_wd[i]),
                        ln_g[i, 1], ln_b[i, 1])
        if not last:
            cs = layer_norm(ALPHA * cs + cg1 * yc, ln_g[i, 0], ln_b[i, 0])
            cs = layer_norm(ALPHA * cs + cg2 * swiglu(cs * (1.0 + csc2) + csh2, ffn_wg[i], ffn_wu[i], ffn_wd[i]),
                            ln_g[i, 1], ln_b[i, 1])
    return xs
```

```python
import functools
import math

import jax
import jax.numpy as jnp
import numpy as np
from jax import lax
from jax.experimental import pallas as pl
from jax.experimental.pallas import tpu as pltpu

F32 = jnp.float32
BF16 = jnp.bfloat16

GRID_W = 64
DEPTH = 2
ALPHA = (2.0 * DEPTH) ** 0.25
LN_EPS = 1e-6
RMS_EPS = 1e-6
HY_EMB = 33
HY_BANDS = (HY_EMB - 1) // 2
HY_FILT_W = 64
HY_TARGET = 1e-2
HY_FAST = 0.3
HY_SLOW = 1.5
MLA_HEADS = 8
Q_LORA = 384
KV_LORA = 256
QK_NOPE = 128
QK_ROPE = 64
V_DIM = 128
QK_DIM = QK_NOPE + QK_ROPE
ROPE_BASE = 10000.0
ATTN_SCALE = QK_DIM ** -0.5

LANES = 128
VMEM_LIMIT_BYTES = 56 * 1024 * 1024

TOKEN_BLOCK = 512
INPROJ_CH_BLOCK = 512
CONV_CH_BLOCK = 64
CONV_CH_CHUNK = 8
FILT_POS_BLOCK = 1024
ATTN_Q_BLOCK = 512
ATTN_K_BLOCK = 1024


def _params(n_axes):
    return pltpu.CompilerParams(dimension_semantics=("arbitrary",) * n_axes,
                                vmem_limit_bytes=VMEM_LIMIT_BYTES)


def _dot(a, b):
    return jnp.dot(a, b, preferred_element_type=F32)


def _dot_nt(a, b):
    return lax.dot_general(a, b, (((1,), (1,)), ((), ())), preferred_element_type=F32)


def _dot_tn(a, b):
    return lax.dot_general(a, b, (((0,), (0,)), ((), ())), preferred_element_type=F32)


def _split_bf16(a):
    hi = a.astype(BF16)
    lo = (a - hi.astype(F32)).astype(BF16)
    return hi, lo


def _dot3(a, b):
    a_hi, a_lo = _split_bf16(a)
    b_hi, b_lo = _split_bf16(b)
    return _dot(a_hi, b_hi) + _dot(a_hi, b_lo) + _dot(a_lo, b_hi)


def _layer_norm(r, g, b):
    mu = jnp.mean(r, axis=-1, keepdims=True)
    d = r - mu
    var = jnp.mean(d * d, axis=-1, keepdims=True)
    return d * lax.rsqrt(var + LN_EPS) * g + b


def _rms_norm(x, g):
    return x * lax.rsqrt(jnp.mean(x * x, axis=-1, keepdims=True) + RMS_EPS) * g


def _mod_kernel(c_ref, w_ref, b_ref, o_ref):
    a = c_ref[...]
    a = a * jax.nn.sigmoid(a)
    o_ref[0] = _dot3(a, w_ref[0]) + b_ref[0]


def _modulation(cond, mod_w, mod_b):
    rows, d = cond.shape
    n6 = mod_w.shape[-1]
    bn = n6 // 4
    out = pl.pallas_call(
        _mod_kernel,
        grid=(DEPTH, n6 // bn),
        in_specs=[pl.BlockSpec((rows, d), lambda i, j: (0, 0)),
                  pl.BlockSpec((1, d, bn), lambda i, j: (i, 0, j)),
                  pl.BlockSpec((1, 1, bn), lambda i, j: (i, 0, j))],
        out_specs=pl.BlockSpec((1, rows, bn), lambda i, j: (i, 0, j)),
        out_shape=jax.ShapeDtypeStruct((DEPTH, rows, n6), F32),
        compiler_params=_params(2),
        name="modulation",
    )(cond, mod_w, mod_b.reshape(DEPTH, 1, n6))
    return out.reshape(DEPTH, rows, 6, d)


def _inproj_kernel(x_ref, mod_ref, wt_ref, b_ref, o_ref, h_scr):
    @pl.when(pl.program_id(2) == 0)
    def _():
        sh = mod_ref[0, 0:1, :]
        sc = mod_ref[0, 1:2, :]
        h_scr[...] = (x_ref[0] * (1.0 + sc) + sh).astype(BF16)

    acc = _dot_nt(wt_ref[...], h_scr[...])
    o_ref[0] = (acc + b_ref[...]).astype(o_ref.dtype)


def _hyena_inproj(x, mod, mod_row0, w_in_t, b_in_col):
    bsz, seq, d = x.shape
    n_out = w_in_t.shape[0]
    tm = min(TOKEN_BLOCK, seq)
    bn = INPROJ_CH_BLOCK
    return pl.pallas_call(
        _inproj_kernel,
        grid=(bsz, seq // tm, n_out // bn),
        in_specs=[pl.BlockSpec((1, tm, d), lambda b, i, j: (b, i, 0)),
                  pl.BlockSpec((1, 6, d), lambda b, i, j: (mod_row0(b), 0, 0)),
                  pl.BlockSpec((bn, d), lambda b, i, j: (j, 0)),
                  pl.BlockSpec((bn, 1), lambda b, i, j: (j, 0))],
        out_specs=pl.BlockSpec((1, bn, tm), lambda b, i, j: (b, j, i)),
        out_shape=jax.ShapeDtypeStruct((bsz, n_out, seq), BF16),
        scratch_shapes=[pltpu.VMEM((tm, d), BF16)],
        compiler_params=_params(3),
        name="hyena_inproj",
    )(x, mod, w_in_t, b_in_col)


def _filter_kernel(fw1_ref, fb1_ref, fw2_ref, fb2_ref, freq_ref, fw3t_ref, absd_ref, o_ref, *, seq, tn):
    blk = pl.program_id(1)
    base = blk * tn
    n_col = base + lax.broadcasted_iota(jnp.int32, (tn, 1), 0)
    t_idx = jnp.where(n_col < seq, n_col, 2 * seq - n_col).astype(F32)
    t_col = t_idx / float(seq - 1)
    w_col = (2.0 * math.pi) * t_idx / float(seq)
    lane = lax.broadcasted_iota(jnp.int32, (1, LANES), 1)
    band = jnp.where(lane <= HY_BANDS, lane - 1, lane - 1 - HY_BANDS).astype(F32)
    f_row = 1e-4 + band * ((HY_BANDS - 1 - 1e-4) / (HY_BANDS - 1))
    arg = w_col * f_row
    feats = jnp.where(lane == 0, t_col,
                      jnp.where(lane <= HY_BANDS, jnp.cos(arg),
                                jnp.where(lane <= 2 * HY_BANDS, -jnp.sin(arg), 0.0)))
    h1 = jnp.sin(freq_ref[0:1, :] * (_dot3(feats, fw1_ref[...]) + fb1_ref[...]))
    h2 = jnp.sin(freq_ref[1:2, :] * (_dot3(h1, fw2_ref[...]) + fb2_ref[...]))
    kt = _dot_nt(fw3t_ref[0, 0], h2.astype(BF16))
    n_row = base + lax.broadcasted_iota(jnp.int32, (1, tn), 1)
    t_row = jnp.where(n_row < seq, n_row, 2 * seq - n_row).astype(F32) / float(seq - 1)
    decay = jnp.exp(-t_row * absd_ref[...])
    o_ref[0] = jnp.where(n_row == seq, 0.0, kt * decay).astype(o_ref.dtype)


def _hyena_filters_t(seq, fw1p, fb1p, fw2p, fb2p, freqp, fw3t, absd_col):
    d = fw3t.shape[2]
    tn = min(FILT_POS_BLOCK, seq)
    nblk = 2 * seq // tn
    half = seq // tn
    return pl.pallas_call(
        functools.partial(_filter_kernel, seq=seq, tn=tn),
        grid=(2, nblk),
        in_specs=[pl.BlockSpec((LANES, LANES), lambda o, i: (0, 0)),
                  pl.BlockSpec((1, LANES), lambda o, i: (0, 0)),
                  pl.BlockSpec((LANES, LANES), lambda o, i: (0, 0)),
                  pl.BlockSpec((1, LANES), lambda o, i: (0, 0)),
                  pl.BlockSpec((2, LANES), lambda o, i: (0, 0)),
                  pl.BlockSpec((1, 1, d, LANES), lambda o, i: (o, i // half, 0, 0)),
                  pl.BlockSpec((d, 1), lambda o, i: (0, 0))],
        out_specs=pl.BlockSpec((1, d, tn), lambda o, i: (o, 0, i)),
        out_shape=jax.ShapeDtypeStruct((2, d, 2 * seq), BF16),
        compiler_params=_params(2),
        name="hyena_filters",
    )(fw1p, fb1p, fw2p, fb2p, freqp, fw3t, absd_col)


def _dft_tables(n1):
    n2 = LANES
    n = n1 * n2
    a1 = 2.0 * np.pi * np.outer(np.arange(n1), np.arange(n1)) / n1
    c1, s1 = np.cos(a1), np.sin(a1)
    a2 = 2.0 * np.pi * np.outer(np.arange(n2), np.arange(n2)) / n2
    c2, s2 = np.cos(a2), np.sin(a2)
    at = 2.0 * np.pi * np.outer(np.arange(n1), np.arange(n2)) / n
    half = n1 // 2
    tabs = dict(
        l1_full=np.concatenate([c1, -s1], axis=0),
        l1_half=np.concatenate([c1[:, :half], -s1[:, :half]], axis=0),
        w2=np.block([[c2, -s2], [s2, c2]]),
        w3=np.block([[c2, s2], [-s2, c2]]),
        l4=np.concatenate([c1[:half, :], -s1[:half, :]], axis=1),
        twr=np.cos(at), twi=-np.sin(at),
    )
    out = {}
    for k, v in tabs.items():
        v = jnp.asarray(v.astype(np.float32))
        out[k] = v if k in ("twr", "twi") else v.astype(BF16)
    return out


def _lane_cat(a, b):
    return jnp.concatenate([a, b], axis=1)


def _spectrum_kernel(kt_ref, l1_ref, w2_ref, twr_ref, twi_ref, o_ref, *, n1, dblk):
    twr = twr_ref[...]
    twi = twi_ref[...]
    inv_n = 1.0 / float(n1 * LANES)

    def pair(i, carry):
        d0 = pl.multiple_of(2 * i, 2)
        r = _dot(l1_ref[...], _lane_cat(kt_ref[0, d0], kt_ref[0, d0 + 1]))
        lhs = []
        for c in range(2):
            pr = r[:n1, c * LANES:(c + 1) * LANES]
            pi = r[n1:, c * LANES:(c + 1) * LANES]
            lhs.append(_lane_cat(pr * twr - pi * twi, pr * twi + pi * twr).astype(BF16))
        x = _dot(jnp.concatenate(lhs, axis=0), w2_ref[...]) * inv_n
        o_ref[0, d0] = x[:n1].astype(o_ref.dtype)
        o_ref[0, d0 + 1] = x[n1:].astype(o_ref.dtype)
        return carry

    lax.fori_loop(0, dblk // 2, pair, 0)


def _filter_spectrum(kt4, tabs, n1):
    _, d, _, _ = kt4.shape
    dblk = CONV_CH_BLOCK
    return pl.pallas_call(
        functools.partial(_spectrum_kernel, n1=n1, dblk=dblk),
        grid=(2, d // dblk),
        in_specs=[pl.BlockSpec((1, dblk, n1, LANES), lambda o, j: (o, j, 0, 0)),
                  pl.BlockSpec((2 * n1, n1), lambda o, j: (0, 0)),
                  pl.BlockSpec((2 * LANES, 2 * LANES), lambda o, j: (0, 0)),
                  pl.BlockSpec((n1, LANES), lambda o, j: (0, 0)),
                  pl.BlockSpec((n1, LANES), lambda o, j: (0, 0))],
        out_specs=pl.BlockSpec((1, dblk, n1, 2 * LANES), lambda o, j: (o, j, 0, 0)),
        out_shape=jax.ShapeDtypeStruct((2, d, n1, 2 * LANES), BF16),
        compiler_params=_params(2),
        name="hyena_filter_spectrum",
    )(kt4, tabs["l1_full"], tabs["w2"], tabs["twr"], tabs["twi"])


PAR_ROWS = 16


def _short_conv(z, p, row0, rows_per_ch):
    r, _ = z.shape
    lane = lax.broadcasted_iota(jnp.int32, (r, LANES), 1)
    sub = lax.broadcasted_iota(jnp.int32, (r, LANES), 0) % rows_per_ch
    a = pltpu.roll(z, 1, axis=1)
    prev = jnp.where(lane == 0, pltpu.roll(a, 1, axis=0), a)
    prev = jnp.where((lane == 0) & (sub == 0), 0.0, prev)
    b = pltpu.roll(z, LANES - 1, axis=1)
    nxt = jnp.where(lane == LANES - 1, pltpu.roll(b, r - 1, axis=0), b)
    nxt = jnp.where((lane == LANES - 1) & (sub == rows_per_ch - 1), 0.0, nxt)
    return prev * p(row0) + z * p(row0 + 1) + nxt * p(row0 + 2) + p(row0 + 3)


def _conv_kernel(z_ref, g_ref, kf_ref, par_ref, l1_ref, w2_ref, w3_ref, l4_ref, twr_ref, twi_ref, o_ref,
                 *, n1, dblk, ndc, conv_z):
    half = n1 // 2
    twr = twr_ref[...]
    twi = twi_ref[...]

    def chunk(ci, carry):
        c0 = pl.multiple_of(ci * ndc, ndc)
        par = par_ref[pl.ds(c0, ndc)]

        def p(k):
            return jnp.broadcast_to(par[:, k:k + 1, :], (ndc, half, LANES)).reshape(ndc * half, LANES)

        z = z_ref[0, pl.ds(c0, ndc)].astype(F32).reshape(ndc * half, LANES)
        g = g_ref[0, pl.ds(c0, ndc)].astype(F32).reshape(ndc * half, LANES)
        if conv_z:
            z = _short_conv(z, p, 0, half)
        g = _short_conv(g, p, 4, half)
        zb = z.astype(BF16)

        lhs2 = []
        for c in range(0, ndc, 2):
            rhs = _lane_cat(zb[c * half:(c + 1) * half], zb[(c + 1) * half:(c + 2) * half])
            r = _dot(l1_ref[...], rhs)
            for cc in range(2):
                pr = r[:n1, cc * LANES:(cc + 1) * LANES]
                pi = r[n1:, cc * LANES:(cc + 1) * LANES]
                lhs2.append(_lane_cat(pr * twr - pi * twi, pr * twi + pi * twr).astype(BF16))
        x = _dot(jnp.concatenate(lhs2, axis=0), w2_ref[...])
        kf = kf_ref[0, pl.ds(c0, ndc)].astype(F32).reshape(ndc * n1, 2 * LANES)
        xr, xi = x[:, :LANES], x[:, LANES:]
        kr, ki = kf[:, :LANES], kf[:, LANES:]
        y = _lane_cat(xr * kr - xi * ki, xr * ki + xi * kr).astype(BF16)
        q = _dot(y, w3_ref[...])
        outs = []
        for c in range(0, ndc, 2):
            qs = []
            for cc in range(2):
                blk = q[(c + cc) * n1:(c + cc + 1) * n1]
                qr, qi = blk[:, :LANES], blk[:, LANES:]
                qs.append((qr * twr + qi * twi, qi * twr - qr * twi))
            rhs = jnp.concatenate([_lane_cat(qs[0][0], qs[1][0]), _lane_cat(qs[0][1], qs[1][1])],
                                  axis=0).astype(BF16)
            yy = _dot(l4_ref[...], rhs)
            outs.append(yy[:, :LANES])
            outs.append(yy[:, LANES:])
        yt = jnp.concatenate(outs, axis=0)
        res = g * (yt + p(8) * z)
        o_ref[0, pl.ds(c0, ndc)] = res.reshape(ndc, half, LANES).astype(o_ref.dtype)
        return carry

    lax.fori_loop(0, dblk // ndc, chunk, 0)


def _long_conv(z4, z_off, g4, g_off, kf, order, par, tabs, n1, conv_z):
    bsz = z4.shape[0]
    d = par.shape[0]
    half = n1 // 2
    dblk = CONV_CH_BLOCK
    nj = d // dblk
    const = lambda j, b: (0, 0)
    return pl.pallas_call(
        functools.partial(_conv_kernel, n1=n1, dblk=dblk, ndc=CONV_CH_CHUNK, conv_z=conv_z),
        grid=(nj, bsz),
        in_specs=[pl.BlockSpec((1, dblk, half, LANES), lambda j, b: (b, z_off * nj + j, 0, 0)),
                  pl.BlockSpec((1, dblk, half, LANES), lambda j, b: (b, g_off * nj + j, 0, 0)),
                  pl.BlockSpec((1, dblk, n1, 2 * LANES), lambda j, b: (order, j, 0, 0)),
                  pl.BlockSpec((dblk, PAR_ROWS, LANES), lambda j, b: (j, 0, 0)),
                  pl.BlockSpec((2 * n1, half), const),
                  pl.BlockSpec((2 * LANES, 2 * LANES), const),
                  pl.BlockSpec((2 * LANES, 2 * LANES), const),
                  pl.BlockSpec((half, 2 * n1), const),
                  pl.BlockSpec((n1, LANES), const),
                  pl.BlockSpec((n1, LANES), const)],
        out_specs=pl.BlockSpec((1, dblk, half, LANES), lambda j, b: (b, j, 0, 0)),
        out_shape=jax.ShapeDtypeStruct((bsz, d, half, LANES), BF16),
        compiler_params=_params(2),
        name="hyena_long_conv",
    )(z4, g4, kf, par, tabs["l1_half"], tabs["w2"], tabs["w3"], tabs["l4"], tabs["twr"], tabs["twi"])


def _ctx_short_conv(u, w, cb):
    _, s = u.shape
    lane = lax.broadcasted_iota(jnp.int32, u.shape, 1)
    prev = jnp.where(lane == 0, 0.0, pltpu.roll(u, 1, axis=1))
    nxt = jnp.where(lane == s - 1, 0.0, pltpu.roll(u, s - 1, axis=1))
    return prev * w[:, 0:1] + u * w[:, 1:2] + nxt * w[:, 2:3] + cb


def _ctx_core_kernel(x1_ref, x2_ref, v_ref, kt_ref, w1_ref, w2_ref, wv_ref, cb_ref, fb_ref,
                     fz_ref, fk_ref, fi_ref, o_ref, *, seq):
    inv_n = 1.0 / float(2 * seq)
    x1 = _ctx_short_conv(x1_ref[0].astype(F32), w1_ref[...], cb_ref[:, 0:1])
    x2 = _ctx_short_conv(x2_ref[0].astype(F32), w2_ref[...], cb_ref[:, 1:2])
    z = _ctx_short_conv(v_ref[0].astype(F32), wv_ref[...], cb_ref[:, 2:3])
    nf = 2 * seq
    for order, gate in enumerate((x1, x2)):
        zf = _dot(z.astype(BF16), fz_ref[...])
        kf = _dot(kt_ref[order], fk_ref[...]) * inv_n
        zr, zi = zf[:, :nf], zf[:, nf:]
        kr, ki = kf[:, :nf], kf[:, nf:]
        y = _lane_cat(zr * kr - zi * ki, zr * ki + zi * kr).astype(BF16)
        conv = _dot(y, fi_ref[...])
        z = gate * (conv + fb_ref[:, order:order + 1] * z)
    o_ref[0] = z.astype(o_ref.dtype)


def _ctx_hyena_core(ut, kt, conv_w3, conv_b3, fbias_t):
    bsz, d3, seq = ut.shape
    d = d3 // 3
    cb = 256
    nj = d // cb
    nf = 2 * seq
    ang = 2.0 * np.pi * np.outer(np.arange(nf), np.arange(nf)) / nf
    c, s = np.cos(ang), np.sin(ang)
    fz = jnp.asarray(np.concatenate([c[:seq], -s[:seq]], axis=1).astype(np.float32)).astype(BF16)
    fk = jnp.asarray(np.concatenate([c, -s], axis=1).astype(np.float32)).astype(BF16)
    fi = jnp.asarray(np.concatenate([c[:, :seq], -s[:, :seq]], axis=0).astype(np.float32)).astype(BF16)
    const = lambda b, j: (0, 0)
    return pl.pallas_call(
        functools.partial(_ctx_core_kernel, seq=seq),
        grid=(bsz, nj),
        in_specs=[pl.BlockSpec((1, cb, seq), lambda b, j: (b, j, 0)),
                  pl.BlockSpec((1, cb, seq), lambda b, j: (b, nj + j, 0)),
                  pl.BlockSpec((1, cb, seq), lambda b, j: (b, 2 * nj + j, 0)),
                  pl.BlockSpec((2, cb, nf), lambda b, j: (0, j, 0)),
                  pl.BlockSpec((cb, 3), lambda b, j: (j, 0)),
                  pl.BlockSpec((cb, 3), lambda b, j: (nj + j, 0)),
                  pl.BlockSpec((cb, 3), lambda b, j: (2 * nj + j, 0)),
                  pl.BlockSpec((cb, 3), lambda b, j: (j, 0)),
                  pl.BlockSpec((cb, 2), lambda b, j: (j, 0)),
                  pl.BlockSpec((seq, 2 * nf), const),
                  pl.BlockSpec((nf, 2 * nf), const),
                  pl.BlockSpec((2 * nf, seq), const)],
        out_specs=pl.BlockSpec((1, cb, seq), lambda b, j: (b, j, 0)),
        out_shape=jax.ShapeDtypeStruct((bsz, d, seq), BF16),
        compiler_params=_params(2),
        name="hyena_ctx_core",
    )(ut, ut, ut, kt, conv_w3, conv_w3, conv_w3, conv_b3, fbias_t, fz, fk, fi)


def _outproj_ln_kernel(a_ref, w_ref, bias_ref, x_ref, mod_ref, lng_ref, lnb_ref, o_ref, *, trans_a):
    a = a_ref[0]
    y = _dot_tn(a, w_ref[...]) if trans_a else _dot(a, w_ref[...])
    y = y + bias_ref[...]
    r = ALPHA * x_ref[0] + mod_ref[0, 2:3, :] * y
    o_ref[0] = _layer_norm(r, lng_ref[...], lnb_ref[...])


def _outproj_ln(a, trans_a, w, bias, x, mod, mod_row0, lng, lnb):
    bsz, seq, d = x.shape
    k = w.shape[0]
    tm = min(TOKEN_BLOCK, seq)
    a_spec = (pl.BlockSpec((1, k, tm), lambda b, i: (b, 0, i)) if trans_a
              else pl.BlockSpec((1, tm, k), lambda b, i: (b, i, 0)))
    row = lambda b, i: (0, 0)
    return pl.pallas_call(
        functools.partial(_outproj_ln_kernel, trans_a=trans_a),
        grid=(bsz, seq // tm),
        in_specs=[a_spec,
                  pl.BlockSpec((k, d), row),
                  pl.BlockSpec((1, d), row),
                  pl.BlockSpec((1, tm, d), lambda b, i: (b, i, 0)),
                  pl.BlockSpec((1, 6, d), lambda b, i: (mod_row0(b), 0, 0)),
                  pl.BlockSpec((1, d), row),
                  pl.BlockSpec((1, d), row)],
        out_specs=pl.BlockSpec((1, tm, d), lambda b, i: (b, i, 0)),
        out_shape=jax.ShapeDtypeStruct((bsz, seq, d), F32),
        compiler_params=_params(2),
        name="mixer_outproj_ln",
    )(a, w, bias, x, mod, lng, lnb)


def _ffn_ln_kernel(x_ref, mod_ref, wg_ref, wu_ref, wd_ref, lng_ref, lnb_ref, o_ref, *, n_chunks):
    x = x_ref[0]
    sh = mod_ref[0, 3:4, :]
    sc = mod_ref[0, 4:5, :]
    gate = mod_ref[0, 5:6, :]
    h = (x * (1.0 + sc) + sh).astype(BF16)
    fc = wg_ref.shape[1] // n_chunks
    y = None
    for c in range(n_chunks):
        g = _dot(h, wg_ref[:, c * fc:(c + 1) * fc])
        u = _dot(h, wu_ref[:, c * fc:(c + 1) * fc])
        a = (g * jax.nn.sigmoid(g) * u).astype(BF16)
        part = _dot(a, wd_ref[c * fc:(c + 1) * fc, :])
        y = part if y is None else y + part
    r = ALPHA * x + gate * y
    o_ref[0] = _layer_norm(r, lng_ref[...], lnb_ref[...])


def _ffn_ln(x, mod, mod_row0, wg, wu, wd, lng, lnb):
    bsz, seq, d = x.shape
    f = wg.shape[1]
    tm = min(TOKEN_BLOCK, seq)
    const = lambda b, i: (0, 0)
    resident = dict(pipeline_mode=pl.Buffered(1))
    return pl.pallas_call(
        functools.partial(_ffn_ln_kernel, n_chunks=2),
        grid=(bsz, seq // tm),
        in_specs=[pl.BlockSpec((1, tm, d), lambda b, i: (b, i, 0)),
                  pl.BlockSpec((1, 6, d), lambda b, i: (mod_row0(b), 0, 0)),
                  pl.BlockSpec((d, f), const, **resident),
                  pl.BlockSpec((d, f), const, **resident),
                  pl.BlockSpec((f, d), const, **resident),
                  pl.BlockSpec((1, d), const),
                  pl.BlockSpec((1, d), const)],
        out_specs=pl.BlockSpec((1, tm, d), lambda b, i: (b, i, 0)),
        out_shape=jax.ShapeDtypeStruct((bsz, seq, d), F32),
        compiler_params=_params(2),
        name="ffn_ln",
    )(x, mod, wg, wu, wd, lng, lnb)


def _mla_proj_kernel(x_ref, mod_ref, cos_ref, sin_ref, wlat_ref, gq_ref, gkv_ref,
                     wqn_ref, wqa_ref, wqb_ref, wkn_ref, wv_ref, *out_refs, with_q):
    sh = mod_ref[0, 0:1, :]
    sc = mod_ref[0, 1:2, :]
    h = (x_ref[0] * (1.0 + sc) + sh).astype(BF16)
    lat = _dot(h, wlat_ref[...])
    cos = cos_ref[...]
    sin = sin_ref[...]
    o0 = Q_LORA + KV_LORA
    ckv = _rms_norm(lat[:, Q_LORA:o0], gkv_ref[...]).astype(BF16)
    k_rope = lat[:, o0:o0 + LANES] * cos + lat[:, o0 + LANES:o0 + 2 * LANES] * sin
    k_nope = _dot(ckv, wkn_ref[...])
    v = _dot(ckv, wv_ref[...])
    if with_q:
        q_ref, k_ref, v_ref = out_refs
        qn = _rms_norm(lat[:, :Q_LORA], gq_ref[...]).astype(BF16)
        q_nope = _dot(qn, wqn_ref[...])
        q_ra = _dot(qn, wqa_ref[...])
        q_rb = _dot(qn, wqb_ref[...])
    else:
        k_ref, v_ref = out_refs
    qscale = ATTN_SCALE * math.log2(math.e)
    for hd in range(MLA_HEADS):
        sl = slice(hd * LANES, (hd + 1) * LANES)
        k_ref[0, hd] = _lane_cat(k_nope[:, sl], k_rope)[:, :QK_DIM].astype(k_ref.dtype)
        v_ref[0, hd] = v[:, sl].astype(v_ref.dtype)
        if with_q:
            q_rope = q_ra[:, sl] * cos + q_rb[:, sl] * sin
            q_ref[0, hd] = (_lane_cat(q_nope[:, sl], q_rope)[:, :QK_DIM] * qscale).astype(q_ref.dtype)


def _mla_proj(x, mod, mod_row0, cos_t, sin_t, w, with_q):
    bsz, seq, d = x.shape
    tm = min(TOKEN_BLOCK, seq)
    const = lambda b, i: (0, 0)
    hw = MLA_HEADS * LANES
    n_lat = Q_LORA + KV_LORA + 2 * LANES
    head_spec = lambda width: pl.BlockSpec((1, MLA_HEADS, tm, width), lambda b, i: (b, 0, i, 0))
    out_specs = [head_spec(QK_DIM), head_spec(V_DIM)]
    out_shape = [jax.ShapeDtypeStruct((bsz, MLA_HEADS, seq, QK_DIM), BF16),
                 jax.ShapeDtypeStruct((bsz, MLA_HEADS, seq, V_DIM), BF16)]
    if with_q:
        out_specs = [head_spec(QK_DIM)] + out_specs
        out_shape = [jax.ShapeDtypeStruct((bsz, MLA_HEADS, seq, QK_DIM), BF16)] + out_shape
    return pl.pallas_call(
        functools.partial(_mla_proj_kernel, with_q=with_q),
        grid=(bsz, seq // tm),
        in_specs=[pl.BlockSpec((1, tm, d), lambda b, i: (b, i, 0)),
                  pl.BlockSpec((1, 6, d), lambda b, i: (mod_row0(b), 0, 0)),
                  pl.BlockSpec((tm, LANES), lambda b, i: (i, 0)),
                  pl.BlockSpec((tm, LANES), lambda b, i: (i, 0)),
                  pl.BlockSpec((d, n_lat), const),
                  pl.BlockSpec((1, Q_LORA), const),
                  pl.BlockSpec((1, KV_LORA), const),
                  pl.BlockSpec((Q_LORA, hw), const),
                  pl.BlockSpec((Q_LORA, hw), const),
                  pl.BlockSpec((Q_LORA, hw), const),
                  pl.BlockSpec((KV_LORA, hw), const),
                  pl.BlockSpec((KV_LORA, hw), const)],
        out_specs=out_specs,
        out_shape=out_shape,
        compiler_params=_params(2),
        name="mla_proj_q" if with_q else "mla_proj_kv",
    )(x, mod, cos_t, sin_t, w["wlat"], w["gq"], w["gkv"], w["wqn"], w["wqa"], w["wqb"], w["wkn"], w["wv"])


def _mla_weights(wdq, gq, wuq, wdkv, gkv, wkr, wuk, wuv):
    d = wdq.shape[0]
    hr = QK_ROPE // 2
    pad = jnp.zeros((d, LANES - QK_ROPE), F32)
    kr_a = jnp.concatenate([wkr, pad], axis=1)
    kr_b = jnp.concatenate([-wkr[:, hr:], wkr[:, :hr], pad], axis=1)
    wlat = jnp.concatenate([wdq, wdkv, kr_a, kr_b], axis=1).astype(BF16)
    wuq3 = wuq.reshape(Q_LORA, MLA_HEADS, QK_DIM)
    rope = wuq3[:, :, QK_NOPE:]
    zpad = jnp.zeros((Q_LORA, MLA_HEADS, LANES - QK_ROPE), F32)
    wqa = jnp.concatenate([rope, zpad], axis=2)
    wqb = jnp.concatenate([-rope[:, :, hr:], rope[:, :, :hr], zpad], axis=2)
    hw = MLA_HEADS * LANES
    return dict(
        wlat=wlat, gq=gq.reshape(1, Q_LORA), gkv=gkv.reshape(1, KV_LORA),
        wqn=wuq3[:, :, :QK_NOPE].reshape(Q_LORA, hw).astype(BF16),
        wqa=wqa.reshape(Q_LORA, hw).astype(BF16),
        wqb=wqb.reshape(Q_LORA, hw).astype(BF16),
        wkn=wuk.astype(BF16), wv=wuv.astype(BF16))


def _rope_tables(seq, identity):
    pad = LANES - QK_ROPE
    if identity:
        cos = jnp.concatenate([jnp.ones((seq, QK_ROPE), F32), jnp.zeros((seq, pad), F32)], axis=1)
        return cos, jnp.zeros((seq, LANES), F32)
    rows = seq // GRID_W
    row = jnp.repeat(jnp.arange(rows), GRID_W).astype(F32)
    col = jnp.tile(jnp.arange(GRID_W), rows).astype(F32)
    n = QK_ROPE // 4
    inv = ROPE_BASE ** (-jnp.arange(n, dtype=F32) / n)
    ang = jnp.concatenate([row[:, None] * inv, col[:, None] * inv], axis=-1)
    z = jnp.zeros((seq, pad), F32)
    return (jnp.concatenate([jnp.cos(ang), jnp.cos(ang), z], axis=1),
            jnp.concatenate([jnp.sin(ang), jnp.sin(ang), z], axis=1))


def _attn_kernel(q_ref, kx_ref, vx_ref, kc_ref, vc_ref, o_ref, *, tk):
    q = q_ref[0, 0]
    tq = q.shape[0]
    n_x = kx_ref.shape[2] // tk

    def step(k, v, carry):
        m, l, acc = carry
        s = _dot_nt(q, k)
        m_new = jnp.maximum(m, jnp.max(s, axis=-1, keepdims=True))
        alpha = jnp.exp2(m - m_new)
        p = jnp.exp2(s - m_new)
        l = alpha * l + jnp.sum(p, axis=-1, keepdims=True)
        acc = alpha * acc + _dot(p.astype(BF16), v)
        return m_new, l, acc

    def body(i, carry):
        start = pl.multiple_of(i * tk, tk)
        return step(kx_ref[0, 0, pl.ds(start, tk), :], vx_ref[0, 0, pl.ds(start, tk), :], carry)

    init = (jnp.full((tq, 1), -jnp.inf, F32), jnp.zeros((tq, 1), F32), jnp.zeros((tq, V_DIM), F32))
    carry = lax.fori_loop(0, n_x, body, init)
    _, l, acc = step(kc_ref[0, 0], vc_ref[0, 0], carry)
    o_ref[0] = (acc / l).astype(o_ref.dtype)


def _attention(q, kx, vx, kc, vc):
    bsz, nh, seq, _ = q.shape
    sc = kc.shape[2]
    tq = min(ATTN_Q_BLOCK, seq)
    tk = min(ATTN_K_BLOCK, seq)
    return pl.pallas_call(
        functools.partial(_attn_kernel, tk=tk),
        grid=(bsz, nh, seq // tq),
        in_specs=[pl.BlockSpec((1, 1, tq, QK_DIM), lambda b, h, i: (b, h, i, 0)),
                  pl.BlockSpec((1, 1, seq, QK_DIM), lambda b, h, i: (b, h, 0, 0)),
                  pl.BlockSpec((1, 1, seq, V_DIM), lambda b, h, i: (b, h, 0, 0)),
                  pl.BlockSpec((1, 1, sc, QK_DIM), lambda b, h, i: (b, h, 0, 0)),
                  pl.BlockSpec((1, 1, sc, V_DIM), lambda b, h, i: (b, h, 0, 0))],
        out_specs=pl.BlockSpec((1, tq, V_DIM), lambda b, h, i: (b, i, h)),
        out_shape=jax.ShapeDtypeStruct((bsz, seq, nh * V_DIM), BF16),
        compiler_params=_params(3),
        name="mla_attention",
    )(q, kx, vx, kc, vc)


def _hyena_filter_inputs(fw1, fb1, fw2, fb2, fw3, freq, d):
    def pad2(a, r, c):
        return jnp.pad(a, ((0, r - a.shape[0]), (0, c - a.shape[1])))
    fw3t = jnp.transpose(fw3.reshape(HY_FILT_W, 2, 2, d), (1, 2, 3, 0))
    fw3t = jnp.pad(fw3t, ((0, 0), (0, 0), (0, 0), (0, LANES - HY_FILT_W))).astype(BF16)
    deltas = np.linspace(math.log(HY_TARGET) / HY_SLOW, math.log(HY_TARGET) / HY_FAST, d, dtype=np.float32)
    absd = jnp.asarray(np.abs(deltas).reshape(d, 1))
    return (pad2(fw1, LANES, LANES), pad2(fb1.reshape(1, -1), 1, LANES), pad2(fw2, LANES, LANES),
            pad2(fb2.reshape(1, -1), 1, LANES), pad2(freq, 2, LANES), fw3t, absd)


def _conv_params(conv_w, conv_b, fbias, z_off, g_off, order, d):
    wz = conv_w[:, z_off * d:(z_off + 1) * d]
    wg = conv_w[:, g_off * d:(g_off + 1) * d]
    rows = jnp.concatenate([wz, conv_b[None, z_off * d:(z_off + 1) * d],
                            wg, conv_b[None, g_off * d:(g_off + 1) * d],
                            fbias[order][None, :], jnp.zeros((PAR_ROWS - 9, d), F32)], axis=0)
    return jnp.broadcast_to(rows.T[:, :, None], (d, PAR_ROWS, LANES))


def _hyena_x(x, mod, hy, d):
    bsz, seq, _ = x.shape
    n1 = 2 * seq // LANES
    half = n1 // 2
    ut = _hyena_inproj(x, mod, lambda b: b, hy["w_in_t"], hy["b_in_col"])
    kt = _hyena_filters_t(seq, *hy["filt"])
    tabs = _dft_tables(n1)
    kf = _filter_spectrum(kt.reshape(2, d, n1, LANES), tabs, n1)
    u4 = ut.reshape(bsz, 3 * d, half, LANES)
    par0 = _conv_params(hy["conv_w"], hy["conv_b"], hy["fbias"], 2, 0, 0, d)
    z1 = _long_conv(u4, 2, u4, 0, kf, 0, par0, tabs, n1, conv_z=True)
    par1 = _conv_params(hy["conv_w"], hy["conv_b"], hy["fbias"], 1, 1, 1, d)
    z2 = _long_conv(z1, 0, u4, 1, kf, 1, par1, tabs, n1, conv_z=False)
    return z2.reshape(bsz, d, seq)


def _hyena_ctx(cs, mod, ctx_row, hy, d):
    bsz, seq, _ = cs.shape
    ut = _hyena_inproj(cs, mod, lambda b: ctx_row, hy["w_in_t"], hy["b_in_col"])
    kt = _hyena_filters_t(seq, *hy["filt"])
    conv_w3 = hy["conv_w"].T
    conv_b3 = hy["conv_b"].reshape(3, d).T
    return _ctx_hyena_core(ut, kt, conv_w3, conv_b3, hy["fbias"].T)


def kernel(x, c, ctx, c_ctx, mod_w, mod_b, ln_g, ln_b, hy_w_in, hy_b_in, hy_conv_w, hy_conv_b, hy_fw1, hy_fb1, hy_fw2, hy_fb2, hy_fw3, hy_freq, hy_fbias, hy_w_out, hy_b_out, mla_wdq, mla_gq, mla_wuq, mla_wdkv, mla_gkv, mla_wkr, mla_wuk, mla_wuv, mla_wo, ffn_wg, ffn_wu, ffn_wd):
    bsz, seq, d = x.shape
    ctx_len = ctx.shape[1]
    ctx_row = bsz
    cond_rows = 16
    cond = jnp.concatenate([c, c_ctx[None, :], jnp.zeros((cond_rows - bsz - 1, d), F32)], axis=0)
    mod = _modulation(cond, mod_w, mod_b)
    x_row = lambda b: b
    c_row = lambda b: ctx_row

    def ln(i, k):
        return ln_g[i, k].reshape(1, d), ln_b[i, k].reshape(1, d)

    def ffn(i, stream, row):
        return _ffn_ln(stream, mod[i], row, ffn_wg[i].astype(BF16), ffn_wu[i].astype(BF16),
                       ffn_wd[i].astype(BF16), *ln(i, 1))

    hy = dict(
        w_in_t=hy_w_in[0].T.astype(BF16), b_in_col=hy_b_in[0].reshape(-1, 1),
        conv_w=hy_conv_w[0], conv_b=hy_conv_b[0], fbias=hy_fbias[0],
        filt=_hyena_filter_inputs(hy_fw1[0], hy_fb1[0], hy_fw2[0], hy_fb2[0], hy_fw3[0], hy_freq[0], d))
    w_out = hy_w_out[0].astype(BF16)
    b_out = hy_b_out[0].reshape(1, d)
    z2t = _hyena_x(x, mod[0], hy, d)
    xs = _outproj_ln(z2t, True, w_out, b_out, x, mod[0], x_row, *ln(0, 0))
    xs = ffn(0, xs, x_row)
    zc = _hyena_ctx(ctx, mod[0], ctx_row, hy, d)
    cs = _outproj_ln(zc, True, w_out, b_out, ctx, mod[0], c_row, *ln(0, 0))
    cs = ffn(0, cs, c_row)

    mw = _mla_weights(mla_wdq[0], mla_gq[0], mla_wuq[0], mla_wdkv[0], mla_gkv[0], mla_wkr[0],
                      mla_wuk[0], mla_wuv[0])
    q, kx, vx = _mla_proj(xs, mod[1], x_row, *_rope_tables(seq, False), mw, with_q=True)
    kc, vc = _mla_proj(cs, mod[1], c_row, *_rope_tables(ctx_len, True), mw, with_q=False)
    o = _attention(q, kx, vx, kc, vc)
    xs = _outproj_ln(o, False, mla_wo[0].astype(BF16), jnp.zeros((1, d), F32), xs, mod[1], x_row, *ln(1, 0))
    return ffn(1, xs, x_row)
```
